```python
import jax, jax.numpy as jnp
from jax import lax
import numpy as np

D_MODEL = 1024
BATCH = 8
SEQ = 2048
DEPTH = 1
DEC_BATCH = 128
DEC_SEQ = 8
PAST_LEN = 16384
PAGE_SIZE = 128

D_MIX = D_MODEL
RET_HEADS = 4
RET_DK = D_MIX // 2 // RET_HEADS
RET_DV = RET_DK
RET_CHUNK = 128
ROPE_BASE = 10000.0
GM_HEADS = 4
GM_DIM = D_MIX // 2 // GM_HEADS
GM_CHUNK = 128
RET_W = RET_HEADS * RET_DK
GM_W = GM_HEADS * GM_DIM
D_IN_PROJ = 4 * RET_W + 2 * GM_W
PEER_HEADS = 8
PEER_NKEYS = 128
PEER_EXPERTS = PEER_NKEYS * PEER_NKEYS
PEER_QDIM = 256
PEER_TOPK = 16
PEER_BLOCK = 128
EPS = 1e-6

kernel_name = "hybrid_retention_gmlp_peer_step"


def rmsnorm(x, g):
    xf = x.astype(jnp.float32)
    y = xf * lax.rsqrt(jnp.mean(xf * xf, axis=-1, keepdims=True) + EPS)
    return (y * g.astype(jnp.float32)).astype(x.dtype)


def headnorm(x, g):
    xf = x.astype(jnp.float32)
    mu = jnp.mean(xf, axis=-1, keepdims=True)
    var = jnp.mean(jnp.square(xf - mu), axis=-1, keepdims=True)
    return ((xf - mu) * lax.rsqrt(var + EPS) * g.astype(jnp.float32)).astype(x.dtype)


def rotary(x, pos):
    half = x.shape[-1] // 2
    inv = ROPE_BASE ** (-jnp.arange(half, dtype=jnp.float32) / half)
    ang = pos.astype(jnp.float32)[:, None] * inv[None, :]
    cos = jnp.cos(ang)[None, :, None, :]
    sin = jnp.sin(ang)[None, :, None, :]
    xf = x.astype(jnp.float32)
    x1, x2 = xf[..., :half], xf[..., half:]
    return jnp.concatenate([x1 * cos - x2 * sin, x1 * sin + x2 * cos], axis=-1).astype(x.dtype)


def ret_log_gamma():
    return jnp.log1p(-jnp.power(2.0, -5.0 - jnp.arange(RET_HEADS, dtype=jnp.float32)))


def retention_chunk(q, k, v, S, lg):
    L = q.shape[1]
    idx = jnp.arange(L, dtype=jnp.float32)
    diff = idx[:, None] - idx[None, :]
    causal = diff >= 0
    decay = jnp.where(causal[None], jnp.exp(lg[:, None, None] * jnp.where(causal, diff, 0.0)[None]), 0.0)
    qf, kf, vf = q.astype(jnp.float32), k.astype(jnp.float32), v.astype(jnp.float32)
    scores = jnp.einsum('bihd,bjhd->bhij', qf, kf) * decay[None]
    inner = jnp.einsum('bhij,bjhv->bihv', scores, vf)
    cross = jnp.einsum('bihd,bhdv->bihv', qf, S) * jnp.exp(lg[None, :] * (idx[:, None] + 1.0))[None, :, :, None]
    kdec = jnp.exp(lg[None, :] * (L - 1.0 - idx[:, None]))
    S_new = jnp.exp(lg * L)[None, :, None, None] * S + jnp.einsum('bjhd,bjh,bjhv->bhdv', kf, kdec[None], vf)
    return inner + cross, S_new


def retention_prompt(q, k, v, lg):
    B, T = q.shape[:2]
    nc = T // RET_CHUNK

    def to_chunks(a):
        return a.reshape(B, nc, RET_CHUNK, *a.shape[2:]).swapaxes(0, 1)

    S0 = jnp.zeros((B, RET_HEADS, RET_DK, RET_DV), jnp.float32)

    def step(S, qkv):
        qc, kc, vc = qkv
        o, S = retention_chunk(qc, kc, vc, S, lg)
        return S, o

    S, o = lax.scan(step, S0, (to_chunks(q), to_chunks(k), to_chunks(v)))
    return o.swapaxes(0, 1).reshape(B, T, RET_HEADS, RET_DV), S


def spatial_gate(u, v, w_s, b_s, L):
    B, T = u.shape[:2]
    nc = T // L
    mask = jnp.tril(jnp.ones((L, L), dtype=bool))
    w = jnp.where(mask[None], w_s[:, :L, :L], 0.0).astype(v.dtype)
    vc = v.reshape(B, nc, L, GM_HEADS, GM_DIM)
    y = jnp.einsum('hts,bnshd->bnthd', w, vc) + b_s[:, :L].T[None, None, :, :, None]
    return u * y.reshape(B, T, GM_HEADS, GM_DIM)


def token_mixer(xn, pos, S0, w_in, ret_gn, gm_ln, gm_ws, gm_bs, w_out):
    B, T, _ = xn.shape
    proj = xn @ w_in
    q, k, v, g, u, vg = jnp.split(proj, [RET_W, 2 * RET_W, 3 * RET_W, 4 * RET_W, 4 * RET_W + GM_W], axis=-1)
    q = rotary(q.reshape(B, T, RET_HEADS, RET_DK), pos)
    k = rotary(k.reshape(B, T, RET_HEADS, RET_DK), pos) * (RET_DK ** -0.5)
    v = v.reshape(B, T, RET_HEADS, RET_DV)
    lg = ret_log_gamma()
    if S0 is None:
        o, S = retention_prompt(q, k, v, lg)
    else:
        o, S = retention_chunk(q, k, v, S0.astype(jnp.float32), lg)
    ret = jax.nn.silu(g) * headnorm(o, ret_gn).astype(xn.dtype).reshape(B, T, RET_W)
    ug = jax.nn.gelu(u.reshape(B, T, GM_HEADS, GM_DIM), approximate=False)
    vgn = headnorm(jax.nn.gelu(vg.reshape(B, T, GM_HEADS, GM_DIM), approximate=False), gm_ln)
    L = GM_CHUNK if S0 is None else T
    gm = spatial_gate(ug, vgn, gm_ws, gm_bs, L).reshape(B, T, GM_W)
    out = jnp.concatenate([ret, gm], axis=-1) @ w_out
    return out, S, vgn


def peer(xn, w_q, sub_keys, u_tab, v_tab):
    shp = xn.shape
    xf = xn.reshape(-1, D_MODEL)
    T = xf.shape[0]
    nb = -(-T // PEER_BLOCK)
    pad = nb * PEER_BLOCK - T
    xp = jnp.pad(xf, ((0, pad), (0, 0))).reshape(nb, PEER_BLOCK, D_MODEL)

    def block(xb):
        q = (xb @ w_q).reshape(PEER_BLOCK, PEER_HEADS, 2, PEER_QDIM // 2)
        s = jnp.einsum('thcd,hckd->thck', q, sub_keys)
        sv, si = lax.top_k(s, PEER_TOPK)
        cand = sv[:, :, 0, :, None] + sv[:, :, 1, None, :]
        cand = cand.reshape(PEER_BLOCK, PEER_HEADS, PEER_TOPK * PEER_TOPK)
        fv, fi = lax.top_k(cand, PEER_TOPK)
        i1 = jnp.take_along_axis(si[:, :, 0, :], fi // PEER_TOPK, axis=-1)
        i2 = jnp.take_along_axis(si[:, :, 1, :], fi % PEER_TOPK, axis=-1)
        e = i1 * PEER_NKEYS + i2
        gate = jax.nn.softmax(fv.astype(jnp.float32), axis=-1).astype(xb.dtype)
        act = jax.nn.gelu(jnp.einsum('thkd,td->thk', u_tab[e], xb), approximate=False)
        return jnp.einsum('thk,thkd->td', gate * act, v_tab[e])

    out = lax.map(block, xp).reshape(nb * PEER_BLOCK, D_MODEL)[:T]
    return out.reshape(shp)


def setup_inputs(seed: int = 0) -> dict:
    key = jax.random.key(seed)
    ks = jax.random.split(key, 16)
    f32 = jnp.float32
    nrm = lambda k, shp, s: jax.random.normal(k, shp, f32) * s
    return {
        "x_prompt": nrm(ks[0], (BATCH, SEQ, D_MODEL), 1.0),
        "x_sample": nrm(ks[1], (DEC_BATCH, DEC_SEQ, D_MODEL), 1.0),
        "state_ret": nrm(ks[2], (DEPTH, DEC_BATCH, RET_HEADS, RET_DK, RET_DV), 0.1),
        "norm_mix": 1.0 + nrm(ks[3], (DEPTH, D_MODEL), 0.01),
        "w_in": nrm(ks[4], (DEPTH, D_MODEL, D_IN_PROJ), D_MODEL ** -0.5),
        "ret_gn": 1.0 + nrm(ks[5], (DEPTH, RET_HEADS, RET_DV), 0.01),
        "gm_ln": 1.0 + nrm(ks[6], (DEPTH, GM_HEADS, GM_DIM), 0.01),
        "gm_ws": nrm(ks[7], (DEPTH, GM_HEADS, GM_CHUNK, GM_CHUNK), 0.1),
        "gm_bs": 1.0 + nrm(ks[8], (DEPTH, GM_HEADS, GM_CHUNK), 0.01),
        "w_out": nrm(ks[9], (DEPTH, D_MIX, D_MODEL), D_MIX ** -0.5),
        "norm_ffn": 1.0 + nrm(ks[10], (DEPTH, D_MODEL), 0.01),
        "peer_wq": nrm(ks[11], (DEPTH, D_MODEL, PEER_HEADS * PEER_QDIM), D_MODEL ** -0.5),
        "peer_keys": nrm(ks[12], (DEPTH, PEER_HEADS, 2, PEER_NKEYS, PEER_QDIM // 2), (PEER_QDIM // 2) ** -0.5),
        "peer_u": nrm(ks[13], (DEPTH, PEER_EXPERTS, D_MODEL), D_MODEL ** -0.5),
        "peer_v": nrm(ks[14], (DEPTH, PEER_EXPERTS, D_MODEL), 0.5),
        "norm_final": 1.0 + nrm(ks[15], (D_MODEL,), 0.01),
    }


def reference(x_prompt, x_sample, state_ret, norm_mix, w_in, ret_gn, gm_ln, gm_ws, gm_bs, w_out,
              norm_ffn, peer_wq, peer_keys, peer_u, peer_v, norm_final):
    pos_p = jnp.arange(SEQ, dtype=jnp.int32)
    pos_s = PAST_LEN + jnp.arange(DEC_SEQ, dtype=jnp.int32)
    hp, hs = x_prompt, x_sample
    st_p, st_s, gv_s = [], [], []
    for l in range(DEPTH):
        mp, Sp, _ = token_mixer(rmsnorm(hp, norm_mix[l]), pos_p, None, w_in[l], ret_gn[l], gm_ln[l],
                                gm_ws[l], gm_bs[l], w_out[l])
        ms, Ss, vs = token_mixer(rmsnorm(hs, norm_mix[l]), pos_s, state_ret[l], w_in[l], ret_gn[l], gm_ln[l],
                                 gm_ws[l], gm_bs[l], w_out[l])
        hp = hp + mp
        hs = hs + ms
        hp = hp + peer(rmsnorm(hp, norm_ffn[l]), peer_wq[l], peer_keys[l], peer_u[l], peer_v[l])
        hs = hs + peer(rmsnorm(hs, norm_ffn[l]), peer_wq[l], peer_keys[l], peer_u[l], peer_v[l])
        st_p.append(Sp.astype(x_prompt.dtype))
        st_s.append(Ss.astype(state_ret.dtype))
        gv_s.append(vs)
    y_prompt = rmsnorm(hp, norm_final)
    y_sample = rmsnorm(hs, norm_final)
    state_ret_prompt = jnp.stack(st_p, axis=0)
    state_ret_sample = jnp.stack(st_s, axis=0)
    gm_v_sample = jnp.stack(gv_s, axis=0)
    return (y_prompt, y_sample, state_ret_prompt, state_ret_sample, gm_v_sample)
```

```python
import functools
import math

import jax
import jax.numpy as jnp
from jax import lax
from jax.experimental import pallas as pl
from jax.experimental.pallas import tpu as pltpu

F32 = jnp.float32
BF16 = jnp.bfloat16

D_MODEL = 1024
SEQ = 2048
PAST_LEN = 16384
DEC_SEQ = 8
N_HEADS = 4
HEAD_DIM = 128
CHUNK = 128
ROPE_BASE = 10000.0
D_IN_PROJ = 6 * N_HEADS * HEAD_DIM
PEER_HEADS = 8
PEER_NKEYS = 128
PEER_TOPK = 16
PEER_QW = PEER_HEADS * 2 * 128
EPS = 1e-6

V7X_LANES = 128
V7X_VMEM_BYTES = 64 * 1024 * 1024

PROMPT_SEQS_PER_STEP = 4
SAMPLE_SEQS_PER_STEP = CHUNK // DEC_SEQ
PREP_TOKENS = 256
DENSE_TOKENS = 512
DENSE_EXPERTS = 1024

_CELLS = [(a, b) for a in range(PEER_TOPK) for b in range(PEER_TOPK) if (a + 1) * (b + 1) <= PEER_TOPK]
_CELL_ROWS = 56


def _vmem_limit(block_bytes, scratch_bytes):
    est = 2 * block_bytes + scratch_bytes
    return int(min(V7X_VMEM_BYTES - (4 << 20), est + max(est // 2, 8 << 20)))


def _nbytes(shape, dtype):
    return math.prod(shape) * jnp.dtype(dtype).itemsize


def _rmsnorm(x, g):
    return x * lax.rsqrt(jnp.mean(x * x, axis=-1, keepdims=True) + EPS) * g


def _headnorm(x, g):
    mu = jnp.mean(x, axis=-1, keepdims=True)
    xc = x - mu
    var = jnp.mean(xc * xc, axis=-1, keepdims=True)
    return xc * lax.rsqrt(var + EPS) * g


def _gelu(x):
    return 0.5 * x * (1.0 + lax.erf(x * (2.0 ** -0.5)))


def _rotary(x, cos_full, sin_signed):
    return x * cos_full + pltpu.roll(x, HEAD_DIM // 2, 1) * sin_signed


def _dot(a, b):
    return jnp.dot(a, b, preferred_element_type=F32)


def _dot_nt(a, b):
    return lax.dot_general(a, b, (((1,), (1,)), ((), ())), preferred_element_type=F32)


def _head_tiles(proj_ref, rows, h):
    w = N_HEADS * HEAD_DIM
    return [proj_ref[rows, i * w + h * HEAD_DIM:i * w + (h + 1) * HEAD_DIM] for i in range(6)]


def _retention_scores(q, k, v, cos, sin, decay):
    qb = _rotary(q, cos, sin).astype(BF16)
    kr = _rotary(k, cos, sin) * (HEAD_DIM ** -0.5)
    vb = v.astype(BF16)
    scores = _dot_nt(qb, kr.astype(BF16)) * decay
    return qb, kr, vb, _dot(scores.astype(BF16), vb)


def _gate_and_gmlp(o, g, u, vg, gn, ln, wmat, bias):
    ret = g * jax.nn.sigmoid(g) * _headnorm(o, gn)
    vgn = _headnorm(_gelu(vg), ln)
    y = _dot(wmat.astype(BF16), vgn.astype(BF16)) + bias
    return ret, _gelu(u) * y, vgn


def _causal_mask(seg):
    ii = lax.broadcasted_iota(jnp.int32, (CHUNK, CHUNK), 0)
    jj = lax.broadcasted_iota(jnp.int32, (CHUNK, CHUNK), 1)
    shift = int(math.log2(seg))
    return ((ii >> shift) == (jj >> shift)) & (ii >= jj)


def _mixer_prompt_kernel(x_ref, gmix_ref, win_ref, cos_ref, sin_ref, decay_ref, qdec_ref, kdec_ref, sdec_ref,
                         gn_ref, ln_ref, ws_ref, bs_ref, wout_ref, h1_ref, s_ref, proj_ref, mix_ref, *, seqs):
    rows_total = seqs * CHUNK
    x = x_ref[...].reshape(rows_total, D_MODEL)
    xn = _rmsnorm(x, gmix_ref[...]).astype(BF16)
    proj_ref[...] = _dot(xn, win_ref[...])

    @pl.when(pl.program_id(1) == 0)
    def _():
        s_ref[...] = jnp.zeros(s_ref.shape, F32)

    cos = cos_ref[...]
    sin = sin_ref[...]
    mask = _causal_mask(CHUNK)

    def seq_body(s, carry):
        rows = pl.ds(pl.multiple_of(s * CHUNK, CHUNK), CHUNK)
        for h in range(N_HEADS):
            q, k, v, g, u, vg = _head_tiles(proj_ref, rows, h)
            qb, kr, vb, inner = _retention_scores(q, k, v, cos, sin, decay_ref[h])
            state = s_ref[s, h]
            o = inner + _dot(qb, state.astype(BF16)) * qdec_ref[h]
            kd_t = (kr * kdec_ref[h]).T.astype(BF16)
            s_ref[s, h] = sdec_ref[h] * state + _dot(kd_t, vb)
            wmat = jnp.where(mask, ws_ref[h], 0.0)
            ret, gm, _ = _gate_and_gmlp(o, g, u, vg, gn_ref[h:h + 1, :], ln_ref[h:h + 1, :], wmat, bs_ref[h])
            mix_ref[rows, h * HEAD_DIM:(h + 1) * HEAD_DIM] = ret
            mix_ref[rows, (N_HEADS + h) * HEAD_DIM:(N_HEADS + h + 1) * HEAD_DIM] = gm
        return carry

    lax.fori_loop(0, seqs, seq_body, 0)
    out = _dot(mix_ref[...].astype(BF16), wout_ref[...])
    h1_ref[...] = (x_ref[...].reshape(rows_total, D_MODEL) + out).reshape(seqs, CHUNK, D_MODEL)


def _mixer_prompt(x, gmix, win, cos, sin, consts, gn, ln, ws, bs, wout):
    batch = x.shape[0]
    seqs = PROMPT_SEQS_PER_STEP
    n_chunks = SEQ // CHUNK
    decay, qdec, kdec, sdec = consts
    full = lambda a: pl.BlockSpec(a.shape, lambda b, c: (0,) * a.ndim)
    in_specs = [
        pl.BlockSpec((seqs, CHUNK, D_MODEL), lambda b, c: (b, c, 0)),
        full(gmix), full(win),
        pl.BlockSpec((CHUNK, HEAD_DIM), lambda b, c: (c, 0)),
        pl.BlockSpec((CHUNK, HEAD_DIM), lambda b, c: (c, 0)),
        full(decay), full(qdec), full(kdec), full(sdec), full(gn), full(ln), full(ws), full(bs), full(wout),
    ]
    out_specs = [
        pl.BlockSpec((seqs, CHUNK, D_MODEL), lambda b, c: (b, c, 0)),
        pl.BlockSpec((seqs, N_HEADS, HEAD_DIM, HEAD_DIM), lambda b, c: (b, 0, 0, 0)),
    ]
    rows = seqs * CHUNK
    scratch = [pltpu.VMEM((rows, D_IN_PROJ), F32), pltpu.VMEM((rows, D_MODEL), F32)]
    block_bytes = (2 * _nbytes((rows, D_MODEL), F32) + _nbytes(win.shape, BF16) + _nbytes(wout.shape, BF16)
                   + 5 * _nbytes(decay.shape, F32) + _nbytes((seqs, N_HEADS, HEAD_DIM, HEAD_DIM), F32))
    scratch_bytes = _nbytes((rows, D_IN_PROJ), F32) + _nbytes((rows, D_MODEL), F32)
    return pl.pallas_call(
        functools.partial(_mixer_prompt_kernel, seqs=seqs),
        grid=(batch // seqs, n_chunks),
        in_specs=in_specs,
        out_specs=out_specs,
        out_shape=[jax.ShapeDtypeStruct(x.shape, F32),
                   jax.ShapeDtypeStruct((batch, N_HEADS, HEAD_DIM, HEAD_DIM), F32)],
        scratch_shapes=scratch,
        compiler_params=pltpu.CompilerParams(
            dimension_semantics=("arbitrary", "arbitrary"),
            vmem_limit_bytes=_vmem_limit(block_bytes, scratch_bytes)),
        name="mixer_prompt",
    )(x, gmix, win, cos, sin, decay, qdec, kdec, sdec, gn, ln, ws, bs, wout)


def _mixer_sample_kernel(x_ref, gmix_ref, win_ref, cos_ref, sin_ref, decay_ref, qdec_ref, kdec_ref, sdec_ref,
                         gn_ref, ln_ref, ws_ref, bs_ref, wout_ref, s0_ref, h1_ref, s_ref, vgn_ref, proj_ref,
                         mix_ref):
    nseq = SAMPLE_SEQS_PER_STEP
    x = x_ref[...]
    xn = _rmsnorm(x, gmix_ref[...]).astype(BF16)
    proj_ref[...] = _dot(xn, win_ref[...])
    cos = cos_ref[...]
    sin = sin_ref[...]
    mask = _causal_mask(DEC_SEQ)
    col_seq = lax.broadcasted_iota(jnp.int32, (CHUNK, CHUNK), 1) >> int(math.log2(DEC_SEQ))
    rows = pl.ds(0, CHUNK)
    for h in range(N_HEADS):
        q, k, v, g, u, vg = _head_tiles(proj_ref, rows, h)
        decay = jnp.where(mask, decay_ref[h], 0.0)
        qb, kr, vb, inner = _retention_scores(q, k, v, cos, sin, decay)
        states = jnp.concatenate([s0_ref[s, h].astype(BF16) for s in range(nseq)], axis=1)
        cross_all = _dot(qb, states)
        cross = jnp.concatenate(
            [cross_all[s * DEC_SEQ:(s + 1) * DEC_SEQ, s * HEAD_DIM:(s + 1) * HEAD_DIM] for s in range(nseq)], axis=0)
        o = inner + cross * qdec_ref[h]
        kd_t = (kr * kdec_ref[h]).T
        for s in range(nseq):
            kd_s = jnp.where(col_seq == s, kd_t, 0.0).astype(BF16)
            s_ref[s, h] = sdec_ref[h] * s0_ref[s, h] + _dot(kd_s, vb)
        wmat = jnp.where(mask, ws_ref[h], 0.0)
        ret, gm, vgn = _gate_and_gmlp(o, g, u, vg, gn_ref[h:h + 1, :], ln_ref[h:h + 1, :], wmat, bs_ref[h])
        mix_ref[:, h * HEAD_DIM:(h + 1) * HEAD_DIM] = ret
        mix_ref[:, (N_HEADS + h) * HEAD_DIM:(N_HEADS + h + 1) * HEAD_DIM] = gm
        vgn_ref[:, h * HEAD_DIM:(h + 1) * HEAD_DIM] = vgn
    h1_ref[...] = x + _dot(mix_ref[...].astype(BF16), wout_ref[...])


def _mixer_sample(x, gmix, win, cos, sin, consts, gn, ln, ws, bs, wout, s0):
    n_tok = x.shape[0]
    nseq = SAMPLE_SEQS_PER_STEP
    decay, qdec, kdec, sdec = consts
    full = lambda a: pl.BlockSpec(a.shape, lambda i: (0,) * a.ndim)
    state_block = (nseq, N_HEADS, HEAD_DIM, HEAD_DIM)
    in_specs = [
        pl.BlockSpec((CHUNK, D_MODEL), lambda i: (i, 0)),
        full(gmix), full(win), full(cos), full(sin), full(decay), full(qdec), full(kdec), full(sdec),
        full(gn), full(ln), full(ws), full(bs), full(wout),
        pl.BlockSpec(state_block, lambda i: (i, 0, 0, 0)),
    ]
    out_specs = [
        pl.BlockSpec((CHUNK, D_MODEL), lambda i: (i, 0)),
        pl.BlockSpec(state_block, lambda i: (i, 0, 0, 0)),
        pl.BlockSpec((CHUNK, N_HEADS * HEAD_DIM), lambda i: (i, 0)),
    ]
    scratch = [pltpu.VMEM((CHUNK, D_IN_PROJ), F32), pltpu.VMEM((CHUNK, D_MODEL), F32)]
    block_bytes = (2 * _nbytes(state_block, F32) + _nbytes(win.shape, BF16) + _nbytes(wout.shape, BF16)
                   + 3 * _nbytes((CHUNK, D_MODEL), F32) + 8 * _nbytes(decay.shape, F32))
    scratch_bytes = _nbytes((CHUNK, D_IN_PROJ), F32) + _nbytes((CHUNK, D_MODEL), F32)
    return pl.pallas_call(
        _mixer_sample_kernel,
        grid=(n_tok // CHUNK,),
        in_specs=in_specs,
        out_specs=out_specs,
        out_shape=[jax.ShapeDtypeStruct((n_tok, D_MODEL), F32),
                   jax.ShapeDtypeStruct(s0.shape, F32),
                   jax.ShapeDtypeStruct((n_tok, N_HEADS * HEAD_DIM), F32)],
        scratch_shapes=scratch,
        compiler_params=pltpu.CompilerParams(
            dimension_semantics=("arbitrary",),
            vmem_limit_bytes=_vmem_limit(block_bytes, scratch_bytes)),
        name="mixer_sample",
    )(x, gmix, win, cos, sin, decay, qdec, kdec, sdec, gn, ln, ws, bs, wout, s0)


def _extract_top(vals, row_iota, count):
    ranks = jnp.full(vals.shape, float(PEER_TOPK), F32)
    picked = []
    for r in range(count):
        m = jnp.max(vals, axis=0, keepdims=True)
        first = jnp.min(jnp.where(vals == m, row_iota, float(vals.shape[0])), axis=0, keepdims=True)
        hit = row_iota == first
        ranks = jnp.where(hit, float(r), ranks)
        vals = jnp.where(hit, -jnp.inf, vals)
        picked.append(m)
    return ranks, picked


def _peer_prep_kernel(h1_ref, gffn_ref, wq_ref, keys_ref, xn_ref, r2_ref, n1_ref, e1_ref, e2_ref, qp_ref, cand_ref,
                      *, tokens):
    xn = _rmsnorm(h1_ref[...], gffn_ref[...]).astype(BF16)
    xn_ref[...] = xn
    qp = _dot(xn, wq_ref[...])
    for hc in range(2 * PEER_HEADS):
        qp_ref[hc] = qp[:, hc * 128:(hc + 1) * 128].astype(BF16)

    key_iota = lax.broadcasted_iota(jnp.int32, (PEER_NKEYS, V7X_LANES), 0).astype(F32)
    cell_iota = lax.broadcasted_iota(jnp.int32, (_CELL_ROWS, V7X_LANES), 0).astype(F32)
    starts = [min(r for r, (a, _) in enumerate(_CELLS) if a == aa) for aa in range(PEER_TOPK)] + [len(_CELLS)]

    def head_body(h, carry):
        for lc in range(tokens // V7X_LANES):
            tok = slice(lc * V7X_LANES, (lc + 1) * V7X_LANES)
            s1 = _dot_nt(keys_ref[h, 0], qp_ref[2 * h, tok, :])
            s2 = _dot_nt(keys_ref[h, 1], qp_ref[2 * h + 1, tok, :])
            r1, v1 = _extract_top(s1, key_iota, PEER_TOPK)
            r2, v2 = _extract_top(s2, key_iota, PEER_TOPK)
            for r, (a, b) in enumerate(_CELLS):
                cand_ref[r:r + 1, :] = v1[a] + v2[b]
            cand_ref[len(_CELLS):_CELL_ROWS, :] = jnp.full((_CELL_ROWS - len(_CELLS), V7X_LANES), -jnp.inf, F32)
            cand = cand_ref[...]
            cell_rank, _ = _extract_top(cand, cell_iota, PEER_TOPK)
            sel = cell_rank < float(PEER_TOPK)
            z = jnp.sum(jnp.where(sel, jnp.exp(cand - cand[0:1, :]), 0.0), axis=0, keepdims=True)
            n1 = jnp.zeros((PEER_NKEYS, V7X_LANES), F32)
            for a in range(PEER_TOPK):
                in_row = sel & (cell_iota >= float(starts[a])) & (cell_iota < float(starts[a + 1]))
                n_a = jnp.sum(jnp.where(in_row, 1.0, 0.0), axis=0, keepdims=True)
                n1 = n1 + jnp.where(r1 == float(a), n_a, 0.0)
            r2_ref[h, :, tok] = r2
            n1_ref[h, :, tok] = n1
            e1_ref[h, :, tok] = jnp.exp(s1 - v1[0]) / z
            e2_ref[h, :, tok] = jnp.exp(s2 - v2[0])
        return carry

    lax.fori_loop(0, PEER_HEADS, head_body, 0)


def _peer_prep(h1, gffn, wq, keys):
    n_tok = h1.shape[0]
    tokens = PREP_TOKENS
    full = lambda a: pl.BlockSpec(a.shape, lambda i: (0,) * a.ndim)
    table = jax.ShapeDtypeStruct((PEER_HEADS, PEER_NKEYS, n_tok), F32)
    table_spec = pl.BlockSpec((PEER_HEADS, PEER_NKEYS, tokens), lambda i: (0, 0, i))
    scratch = [pltpu.VMEM((2 * PEER_HEADS, tokens, 128), BF16), pltpu.VMEM((_CELL_ROWS, V7X_LANES), F32)]
    block_bytes = (_nbytes((tokens, D_MODEL), F32) + _nbytes((tokens, D_MODEL), BF16) + _nbytes(wq.shape, BF16)
                   + _nbytes(keys.shape, BF16) + 4 * _nbytes((PEER_HEADS, PEER_NKEYS, tokens), F32))
    scratch_bytes = _nbytes((tokens, PEER_QW), F32) + _nbytes((tokens, PEER_QW), BF16)
    return pl.pallas_call(
        functools.partial(_peer_prep_kernel, tokens=tokens),
        grid=(n_tok // tokens,),
        in_specs=[pl.BlockSpec((tokens, D_MODEL), lambda i: (i, 0)), full(gffn), full(wq), full(keys)],
        out_specs=[pl.BlockSpec((tokens, D_MODEL), lambda i: (i, 0)), table_spec, table_spec, table_spec, table_spec],
        out_shape=[jax.ShapeDtypeStruct((n_tok, D_MODEL), BF16), table, table, table, table],
        scratch_shapes=scratch,
        compiler_params=pltpu.CompilerParams(
            dimension_semantics=("arbitrary",),
            vmem_limit_bytes=_vmem_limit(block_bytes, scratch_bytes)),
        name="peer_prep",
    )(h1, gffn, wq, keys)


def _peer_dense_kernel(xn_ref, u_ref, vt_ref, r2_ref, n1_ref, e1_ref, e2_ref, h1_ref, gfin_ref, y_ref, acc_ref,
                       ht_ref, pt_ref, *, tokens, experts):
    j = pl.program_id(1)

    @pl.when(j == 0)
    def _():
        acc_ref[...] = jnp.zeros(acc_ref.shape, F32)

    ht_ref[...] = _dot_nt(u_ref[...], xn_ref[...])
    rows_per_tile = experts // PEER_NKEYS
    assert rows_per_tile == 8, "the i1 rows of a tile must be one aligned sublane group"
    i1_rows = pl.ds(pl.multiple_of(j * rows_per_tile, rows_per_tile), rows_per_tile)
    for ii in range(rows_per_tile):
        sub = slice(ii * PEER_NKEYS, (ii + 1) * PEER_NKEYS)
        for lc in range(tokens // V7X_LANES):
            tok = slice(lc * V7X_LANES, (lc + 1) * V7X_LANES)
            coef = jnp.zeros((PEER_NKEYS, V7X_LANES), F32)
            for h in range(PEER_HEADS):
                n1 = n1_ref[h, i1_rows, tok][ii:ii + 1, :]
                e1 = e1_ref[h, i1_rows, tok][ii:ii + 1, :]
                coef = coef + jnp.where(r2_ref[h, :, tok] < n1, e1 * e2_ref[h, :, tok], 0.0)
            pt_ref[sub, tok] = (coef * _gelu(ht_ref[sub, tok])).astype(BF16)
    acc_ref[...] += _dot(vt_ref[...], pt_ref[...])

    @pl.when(j == pl.num_programs(1) - 1)
    def _():
        y_ref[...] = _rmsnorm(h1_ref[...] + acc_ref[...].T, gfin_ref[...])


def _peer_dense(xn, u, vt, r2, n1, e1, e2, h1, gfin):
    n_tok = xn.shape[0]
    n_exp = u.shape[0]
    tokens, experts = DENSE_TOKENS, DENSE_EXPERTS
    table_spec = pl.BlockSpec((PEER_HEADS, PEER_NKEYS, tokens), lambda i, j: (0, 0, i))
    in_specs = [
        pl.BlockSpec((tokens, D_MODEL), lambda i, j: (i, 0)),
        pl.BlockSpec((experts, D_MODEL), lambda i, j: (j, 0)),
        pl.BlockSpec((D_MODEL, experts), lambda i, j: (0, j)),
        table_spec, table_spec, table_spec, table_spec,
        pl.BlockSpec((tokens, D_MODEL), lambda i, j: (i, 0)),
        pl.BlockSpec(gfin.shape, lambda i, j: (0, 0)),
    ]
    scratch = [pltpu.VMEM((D_MODEL, tokens), F32), pltpu.VMEM((experts, tokens), F32),
               pltpu.VMEM((experts, tokens), BF16)]
    block_bytes = (_nbytes((tokens, D_MODEL), BF16) + 2 * _nbytes((experts, D_MODEL), BF16)
                   + 4 * _nbytes((PEER_HEADS, PEER_NKEYS, tokens), F32) + 2 * _nbytes((tokens, D_MODEL), F32))
    scratch_bytes = (_nbytes((D_MODEL, tokens), F32) + _nbytes((experts, tokens), F32)
                     + _nbytes((experts, tokens), BF16))
    return pl.pallas_call(
        functools.partial(_peer_dense_kernel, tokens=tokens, experts=experts),
        grid=(n_tok // tokens, n_exp // experts),
        in_specs=in_specs,
        out_specs=pl.BlockSpec((tokens, D_MODEL), lambda i, j: (i, 0)),
        out_shape=jax.ShapeDtypeStruct((n_tok, D_MODEL), F32),
        scratch_shapes=scratch,
        compiler_params=pltpu.CompilerParams(
            dimension_semantics=("arbitrary", "arbitrary"),
            vmem_limit_bytes=_vmem_limit(block_bytes, scratch_bytes)),
        name="peer_dense",
    )(xn, u, vt, r2, n1, e1, e2, h1, gfin)


def _rotary_tables(pos):
    half = HEAD_DIM // 2
    inv = ROPE_BASE ** (-jnp.arange(half, dtype=F32) / half)
    ang = pos.astype(F32)[:, None] * inv[None, :]
    cos, sin = jnp.cos(ang), jnp.sin(ang)
    return jnp.concatenate([cos, cos], axis=-1), jnp.concatenate([-sin, sin], axis=-1)


def _retention_tables(seg):
    lg = jnp.log1p(-jnp.power(2.0, -5.0 - jnp.arange(N_HEADS, dtype=F32)))
    row = jnp.arange(CHUNK, dtype=F32)
    idx = jnp.arange(CHUNK, dtype=jnp.int32) % seg
    diff = row[:, None] - row[None, :]
    decay = jnp.exp(lg[:, None, None] * jnp.where(diff >= 0, diff, 0.0)[None])
    decay = jnp.where((diff >= 0)[None], decay, 0.0)
    idxf = idx.astype(F32)
    qdec = jnp.exp(lg[:, None] * (idxf[None, :] + 1.0))
    kdec = jnp.exp(lg[:, None] * (seg - 1.0 - idxf[None, :]))
    sdec = jnp.exp(lg * seg)
    bcast = lambda a: jnp.broadcast_to(a[:, :, None], (N_HEADS, CHUNK, HEAD_DIM))
    return decay, bcast(qdec), bcast(kdec), jnp.broadcast_to(sdec[:, None, None], (N_HEADS, 1, HEAD_DIM))


def kernel(x_prompt, x_sample, state_ret, norm_mix, w_in, ret_gn, gm_ln, gm_ws, gm_bs, w_out, norm_ffn, peer_wq,
           peer_keys, peer_u, peer_v, norm_final):
    assert norm_mix.shape[0] == 1, "single layer"
    batch, dec_batch = x_prompt.shape[0], x_sample.shape[0]
    gmix, gffn, gfin = norm_mix, norm_ffn, norm_final[None, :]
    win, wout, wq = w_in[0].astype(BF16), w_out[0].astype(BF16), peer_wq[0].astype(BF16)
    keys = peer_keys[0].astype(BF16)
    u_tab = peer_u[0].astype(BF16)
    vt_tab = peer_v[0].astype(BF16).T
    gn, ln = ret_gn[0], gm_ln[0]
    reps = CHUNK // DEC_SEQ

    cos_p, sin_p = _rotary_tables(jnp.arange(SEQ, dtype=jnp.int32))
    bs_p = jnp.broadcast_to(gm_bs[0][:, :CHUNK, None], (N_HEADS, CHUNK, HEAD_DIM))
    h1_p, state_p = _mixer_prompt(x_prompt, gmix, win, cos_p, sin_p, _retention_tables(CHUNK), gn, ln,
                                  gm_ws[0], bs_p, wout)

    cos_s, sin_s = _rotary_tables(PAST_LEN + jnp.arange(DEC_SEQ, dtype=jnp.int32))
    cos_s, sin_s = jnp.tile(cos_s, (reps, 1)), jnp.tile(sin_s, (reps, 1))
    ws_s = jnp.tile(gm_ws[0][:, :DEC_SEQ, :DEC_SEQ], (1, reps, reps))
    bs_s = jnp.broadcast_to(jnp.tile(gm_bs[0][:, :DEC_SEQ], (1, reps))[:, :, None], (N_HEADS, CHUNK, HEAD_DIM))
    h1_s, state_s, vgn_s = _mixer_sample(x_sample.reshape(dec_batch * DEC_SEQ, D_MODEL), gmix, win, cos_s, sin_s,
                                         _retention_tables(DEC_SEQ), gn, ln, ws_s, bs_s, wout, state_ret[0])

    n_prompt = batch * SEQ
    h1 = jnp.concatenate([h1_p.reshape(n_prompt, D_MODEL), h1_s], axis=0)
    xn, r2, n1, e1, e2 = _peer_prep(h1, gffn, wq, keys)
    y = _peer_dense(xn, u_tab, vt_tab, r2, n1, e1, e2, h1, gfin)

    y_prompt = y[:n_prompt].reshape(batch, SEQ, D_MODEL)
    y_sample = y[n_prompt:].reshape(dec_batch, DEC_SEQ, D_MODEL)
    gm_v_sample = vgn_s.reshape(1, dec_batch, DEC_SEQ, N_HEADS, HEAD_DIM)
    return (y_prompt, y_sample, state_p[None], state_s[None], gm_v_sample)
```
